```python
import math
import jax, jax.numpy as jnp
from jax import lax
import numpy as np

D_MODEL = 1024
BATCH = 16
SEQ = 2048
DEPTH = 2

CHUNK = 64
Q_BLOCK = 128
N_A_LAYERS = DEPTH // 2
N_B_LAYERS = DEPTH - N_A_LAYERS
A_HEADS = 8
A_HEAD_DIM = D_MODEL // (2 * A_HEADS)
B_HEADS = 16
B_HEAD_DIM = D_MODEL // B_HEADS
LEFT_CHUNKS = 8
BAND_CHUNKS = LEFT_CHUNKS + 1
MAX_REL_DIST = 256
D_FF = -(-8 * D_MODEL // (3 * 256)) * 256
ROPE_THETA = 10000.0
LN_EPS = 1e-5
SUBLN_EPS = 1e-5
DEEPNORM_ALPHA = (2 * DEPTH) ** 0.25
DEEPNORM_BETA = (8 * DEPTH) ** -0.25

kernel_name = "yoco_diffattn_chunkband_deepnorm"


def layer_norm(x, g, b):
    xf = x.astype(jnp.float32)
    mu = jnp.mean(xf, axis=-1, keepdims=True)
    var = jnp.mean(jnp.square(xf - mu), axis=-1, keepdims=True)
    y = (xf - mu) * lax.rsqrt(var + LN_EPS) * g.astype(jnp.float32) + b.astype(jnp.float32)
    return y.astype(x.dtype)


def rms_norm(x, g):
    xf = x.astype(jnp.float32)
    y = xf * lax.rsqrt(jnp.mean(jnp.square(xf), axis=-1, keepdims=True) + SUBLN_EPS)
    return (y * g.astype(jnp.float32)).astype(x.dtype)


def rope_tables(seq, dim):
    inv = ROPE_THETA ** (-jnp.arange(0, dim, 2, dtype=jnp.float32) / dim)
    ang = jnp.arange(seq, dtype=jnp.float32)[:, None] * inv[None, :]
    return jnp.cos(ang), jnp.sin(ang)


def apply_rope(x, cos, sin):
    half = x.shape[-1] // 2
    x1, x2 = x[..., :half], x[..., half:]
    cos = cos.astype(x.dtype)
    sin = sin.astype(x.dtype)
    return jnp.concatenate([x1 * cos - x2 * sin, x1 * sin + x2 * cos], axis=-1)


def lambda_init_fn(layer_idx):
    return 0.8 - 0.6 * math.exp(-0.3 * layer_idx)


def diff_attention(x, w_qkv, lam_params, subln_g, w_o, lambda_init):
    b, s, _ = x.shape
    qkv = x @ w_qkv
    q, k, v = jnp.split(qkv, 3, axis=-1)
    q = q.reshape(b, s, A_HEADS, 2, A_HEAD_DIM)
    k = k.reshape(b, s, A_HEADS, 2, A_HEAD_DIM)
    v = v.reshape(b, s, A_HEADS, 2 * A_HEAD_DIM)
    cos, sin = rope_tables(s, A_HEAD_DIM)
    cos = cos[:, None, None, :]
    sin = sin[:, None, None, :]
    q = apply_rope(q, cos, sin) * (A_HEAD_DIM ** -0.5)
    k = apply_rope(k, cos, sin)
    lp = lam_params.astype(jnp.float32)
    lam = jnp.exp(jnp.sum(lp[0] * lp[1])) - jnp.exp(jnp.sum(lp[2] * lp[3])) + lambda_init
    chunk_id = jnp.arange(s) // CHUNK
    neg = jnp.finfo(jnp.float32).min
    outs = []
    for i in range(s // Q_BLOCK):
        q0 = i * Q_BLOCK
        kend = q0 + Q_BLOCK
        sc = jnp.einsum('bqhmd,bkhmd->bhmqk', q[:, q0:kend], k[:, :kend],
                        preferred_element_type=jnp.float32)
        allowed = chunk_id[None, :kend] <= chunk_id[q0:kend, None]
        p = jax.nn.softmax(jnp.where(allowed, sc, neg), axis=-1)
        a = p[:, :, 0] - lam * p[:, :, 1]
        outs.append(jnp.einsum('bhqk,bkhe->bqhe', a.astype(v.dtype), v[:, :kend]))
    o = jnp.concatenate(outs, axis=1)
    o = rms_norm(o, subln_g) * (1.0 - lambda_init)
    return o.reshape(b, s, D_MODEL) @ w_o


def shared_kv_band(x, kv_w):
    b, s, _ = x.shape
    nc = s // CHUNK
    kv = x @ kv_w
    k, v = jnp.split(kv, 2, axis=-1)
    k = k.reshape(b, nc, CHUNK, B_HEADS, B_HEAD_DIM)
    v = v.reshape(b, nc, CHUNK, B_HEADS, B_HEAD_DIM)
    pad = ((0, 0), (LEFT_CHUNKS, 0), (0, 0), (0, 0), (0, 0))
    kp = jnp.pad(k, pad)
    vp = jnp.pad(v, pad)
    k_band = jnp.concatenate([kp[:, j:j + nc] for j in range(BAND_CHUNKS)], axis=2)
    v_band = jnp.concatenate([vp[:, j:j + nc] for j in range(BAND_CHUNKS)], axis=2)
    band_chunk = jnp.arange(BAND_CHUNKS * CHUNK) // CHUNK
    band_valid = (jnp.arange(nc)[:, None] - LEFT_CHUNKS + band_chunk[None, :]) >= 0
    return k_band, v_band, band_valid


def chunk_attention(x, k_band, v_band, band_valid, w_q, rel_table, w_o):
    b, s, _ = x.shape
    nc = s // CHUNK
    q = (x @ w_q).reshape(b, nc, CHUNK, B_HEADS, B_HEAD_DIM) * (B_HEAD_DIM ** -0.5)
    rel = (jnp.arange(BAND_CHUNKS * CHUNK)[None, :] - LEFT_CHUNKS * CHUNK
           - jnp.arange(CHUNK)[:, None])
    idx = jnp.clip(rel, -MAX_REL_DIST, MAX_REL_DIST) + MAX_REL_DIST
    bias = rel_table[:, idx].astype(jnp.float32)
    sc = jnp.einsum('bnqhd,bnkhd->bhnqk', q, k_band,
                    preferred_element_type=jnp.float32) + bias[:, None]
    sc = jnp.where(band_valid[:, None, :], sc, jnp.finfo(jnp.float32).min)
    p = jax.nn.softmax(sc, axis=-1)
    o = jnp.einsum('bhnqk,bnkhd->bnqhd', p.astype(v_band.dtype), v_band)
    return o.reshape(b, s, D_MODEL) @ w_o


def swiglu(x, w_in, w_out):
    gate, up = jnp.split(x @ w_in, 2, axis=-1)
    return (jax.nn.silu(gate) * up) @ w_out


def setup_inputs(seed: int = 0) -> dict:
    key = jax.random.key(seed)
    ks = jax.random.split(key, 14)
    f32 = jnp.float32
    d = D_MODEL
    nrm = lambda k, shape, sc: jax.random.normal(k, shape, f32) * sc
    return {
        "x": nrm(ks[0], (BATCH, SEQ, d), 1.0),
        "a_w_qkv": nrm(ks[1], (N_A_LAYERS, d, 3 * d), d ** -0.5),
        "a_lambda": nrm(ks[2], (N_A_LAYERS, 4, A_HEAD_DIM), 0.1),
        "a_subln_g": 1.0 + nrm(ks[3], (N_A_LAYERS, 2 * A_HEAD_DIM), 0.02),
        "a_w_o": nrm(ks[4], (N_A_LAYERS, d, d), d ** -0.5 * DEEPNORM_BETA),
        "kv_w": nrm(ks[5], (d, 2 * d), d ** -0.5),
        "b_w_q": nrm(ks[6], (N_B_LAYERS, d, d), d ** -0.5),
        "b_rel_bias": nrm(ks[7], (N_B_LAYERS, B_HEADS, 2 * MAX_REL_DIST + 1), 0.2),
        "b_w_o": nrm(ks[8], (N_B_LAYERS, d, d), d ** -0.5 * DEEPNORM_BETA),
        "ln_g": 1.0 + nrm(ks[9], (DEPTH, 2, d), 0.02),
        "ln_b": nrm(ks[10], (DEPTH, 2, d), 0.02),
        "ffn_w_in": nrm(ks[11], (DEPTH, d, 2 * D_FF), d ** -0.5),
        "ffn_w_out": nrm(ks[12], (DEPTH, D_FF, d), D_FF ** -0.5 * DEEPNORM_BETA),
    }


def reference(x, a_w_qkv, a_lambda, a_subln_g, a_w_o, kv_w, b_w_q, b_rel_bias, b_w_o,
              ln_g, ln_b, ffn_w_in, ffn_w_out):
    k_band = v_band = band_valid = None
    for l in range(DEPTH):
        if l < N_A_LAYERS:
            mix = diff_attention(x, a_w_qkv[l], a_lambda[l], a_subln_g[l], a_w_o[l],
                                 lambda_init_fn(l))
        else:
            if l == N_A_LAYERS:
                k_band, v_band, band_valid = shared_kv_band(x, kv_w)
            j = l - N_A_LAYERS
            mix = chunk_attention(x, k_band, v_band, band_valid, b_w_q[j], b_rel_bias[j], b_w_o[j])
        x = layer_norm(DEEPNORM_ALPHA * x + mix, ln_g[l, 0], ln_b[l, 0])
        x = layer_norm(DEEPNORM_ALPHA * x + swiglu(x, ffn_w_in[l], ffn_w_out[l]), ln_g[l, 1], ln_b[l, 1])
    return x
```

```python
import functools
import math

import jax
import jax.numpy as jnp
from jax import lax
from jax.experimental import pallas as pl
from jax.experimental.pallas import tpu as pltpu

D_MODEL = 1024
SEQ = 2048
DEPTH = 2
CHUNK = 64
N_A_LAYERS = DEPTH // 2
A_HEADS = 8
A_HEAD_DIM = D_MODEL // (2 * A_HEADS)
B_HEADS = 16
B_HEAD_DIM = D_MODEL // B_HEADS
LEFT_CHUNKS = 8
MAX_REL_DIST = 256
D_FF = 2816
ROPE_THETA = 10000.0
LN_EPS = 1e-5
SUBLN_EPS = 1e-5
DEEPNORM_ALPHA = (2 * DEPTH) ** 0.25

F32 = jnp.float32
BF16 = jnp.bfloat16
LANES = 128
MASKED = -1e30
ATTN_TQ = 256
NT_DIMS = (((1,), (1,)), ((), ()))


def _params(semantics, vmem_mib):
    return pltpu.CompilerParams(dimension_semantics=semantics, vmem_limit_bytes=vmem_mib * 2**20)


def _mm_kernel(x_ref, w_ref, o_ref):
    x = x_ref[...].astype(BF16)
    o_ref[...] = jnp.dot(x, w_ref[...], preferred_element_type=F32).astype(o_ref.dtype)


def _mm_rope_kernel(x_ref, w_ref, cos_ref, sin_ref, o_ref):
    x = x_ref[...].astype(BF16)
    acc = jnp.dot(x, w_ref[...], preferred_element_type=F32)
    cos = cos_ref[0]
    sin = sin_ref[0]
    for c in range(acc.shape[1] // LANES):
        a = acc[:, c * LANES:(c + 1) * LANES]
        r = pltpu.roll(a, LANES // 2, 1)
        o_ref[:, c * LANES:(c + 1) * LANES] = (a * cos + r * sin).astype(o_ref.dtype)


def _vt_kernel(x_ref, wt_ref, o_ref):
    x = x_ref[...].astype(BF16)
    o_ref[...] = lax.dot_general(wt_ref[...], x, NT_DIMS, preferred_element_type=F32).astype(o_ref.dtype)


def _project(x2, w, *, tm=1024, tn=1024, rope=None):
    t, d = x2.shape
    n = w.shape[1]
    grid = (t // tm, n // tn)
    in_specs = [pl.BlockSpec((tm, d), lambda i, j: (i, 0)), pl.BlockSpec((d, tn), lambda i, j: (0, j))]
    args = [x2, w]
    kern = _mm_kernel
    if rope is not None:
        assert tn == D_MODEL and n == 2 * D_MODEL
        spb = SEQ // tm
        tab = pl.BlockSpec((1, tm, LANES), lambda i, j: (j, i % spb, 0))
        in_specs += [tab, tab]
        args += list(rope)
        kern = _mm_rope_kernel
    return pl.pallas_call(
        kern,
        grid=grid,
        in_specs=in_specs,
        out_specs=pl.BlockSpec((tm, tn), lambda i, j: (i, j)),
        out_shape=jax.ShapeDtypeStruct((t, n), BF16),
        compiler_params=_params(("parallel", "arbitrary"), 48),
        name="project_rope" if rope is not None else "project",
    )(*args)


def _project_vt(x2, wt, *, tm=1024, tn=1024):
    t, d = x2.shape
    n = wt.shape[0]
    spb = SEQ // tm
    return pl.pallas_call(
        _vt_kernel,
        grid=(t // tm, n // tn),
        in_specs=[pl.BlockSpec((tm, d), lambda i, j: (i, 0)), pl.BlockSpec((tn, d), lambda i, j: (j, 0))],
        out_specs=pl.BlockSpec((None, tn, tm), lambda i, j: (i // spb, j, i % spb)),
        out_shape=jax.ShapeDtypeStruct((t // SEQ, n, SEQ), BF16),
        compiler_params=_params(("parallel", "arbitrary"), 48),
        name="project_vt",
    )(x2, wt)


def _softmax_step(s, vt, m_ref, l_ref, acc_ref, idx):
    m_old = m_ref[idx]
    m_new = jnp.maximum(m_old, jnp.max(s, axis=0, keepdims=True))
    alpha = jnp.exp(m_old - m_new)
    p = jnp.exp(s - m_new)
    l_ref[idx] = alpha * l_ref[idx] + jnp.sum(p, axis=0, keepdims=True)
    acc_ref[idx] = alpha * acc_ref[idx] + jnp.dot(vt, p.astype(BF16), preferred_element_type=F32)
    m_ref[idx] = m_new


def _init_softmax_state(m_ref, l_ref, acc_ref):
    m_ref[...] = jnp.full(m_ref.shape, MASKED, F32)
    l_ref[...] = jnp.zeros(l_ref.shape, F32)
    acc_ref[...] = jnp.zeros(acc_ref.shape, F32)


def _attn_a_kernel(lam_ref, g_ref, q_ref, k_ref, vt_ref, o_ref, m_ref, l_ref, acc_ref, *, lambda_init):
    tq = ATTN_TQ
    qi = pl.program_id(2)
    q = q_ref[...]
    lane = lax.broadcasted_iota(jnp.int32, q.shape, 1)
    is_map0 = (lane & (A_HEAD_DIM // 2)) == 0
    zero = jnp.zeros_like(q)
    q_maps = (jnp.where(is_map0, q, zero), jnp.where(is_map0, zero, q))
    _init_softmax_state(m_ref, l_ref, acc_ref)

    def update(start, mask):
        k = k_ref[pl.ds(start, tq), :]
        vt = vt_ref[:, pl.ds(start, tq)]
        for mp in range(2):
            s = lax.dot_general(k, q_maps[mp], NT_DIMS, preferred_element_type=F32)
            if mask is not None:
                s = jnp.where(mask, s, MASKED)
            _softmax_step(s, vt, m_ref, l_ref, acc_ref, mp)

    def body(kb, carry):
        update(pl.multiple_of(kb * tq, tq), None)
        return carry

    lax.fori_loop(0, qi, body, 0)
    key_chunk = lax.broadcasted_iota(jnp.int32, (tq, tq), 0) // CHUNK
    qry_chunk = lax.broadcasted_iota(jnp.int32, (tq, tq), 1) // CHUNK
    update(pl.multiple_of(qi * tq, tq), key_chunk <= qry_chunk)

    lp = lam_ref[...]
    lam = (jnp.exp(jnp.sum(lp[0:1] * lp[1:2], axis=1, keepdims=True))
           - jnp.exp(jnp.sum(lp[2:3] * lp[3:4], axis=1, keepdims=True)) + lambda_init)
    o = acc_ref[0] * (1.0 / l_ref[0]) - lam * (acc_ref[1] * (1.0 / l_ref[1]))
    ms = jnp.mean(o * o, axis=0, keepdims=True)
    o = o * lax.rsqrt(ms + SUBLN_EPS) * (g_ref[...] * (1.0 - lambda_init))
    o_ref[...] = o.T.astype(o_ref.dtype)


def _diff_attention(lam_params, subln_g, qk, vt, lambda_init):
    t = qk.shape[0]
    b = t // SEQ
    tq = ATTN_TQ
    nq = SEQ // tq
    hd2 = 2 * A_HEAD_DIM
    return pl.pallas_call(
        functools.partial(_attn_a_kernel, lambda_init=lambda_init),
        grid=(b, A_HEADS, nq),
        in_specs=[
            pl.BlockSpec((4, A_HEAD_DIM), lambda bi, h, qi: (0, 0)),
            pl.BlockSpec((hd2, 1), lambda bi, h, qi: (0, 0)),
            pl.BlockSpec((tq, hd2), lambda bi, h, qi: (bi * nq + qi, h)),
            pl.BlockSpec((SEQ, hd2), lambda bi, h, qi: (bi, A_HEADS + h)),
            pl.BlockSpec((None, hd2, SEQ), lambda bi, h, qi: (bi, h, 0)),
        ],
        out_specs=pl.BlockSpec((tq, hd2), lambda bi, h, qi: (bi * nq + qi, h)),
        out_shape=jax.ShapeDtypeStruct((t, D_MODEL), BF16),
        scratch_shapes=[
            pltpu.VMEM((2, 1, tq), F32),
            pltpu.VMEM((2, 1, tq), F32),
            pltpu.VMEM((2, hd2, tq), F32),
        ],
        compiler_params=_params(("parallel", "parallel", "arbitrary"), 32),
        name="diff_attention",
    )(lam_params.astype(F32), subln_g.astype(F32).reshape(hd2, 1), qk, qk, vt)


BAND_BLOCKS = LEFT_CHUNKS * CHUNK // ATTN_TQ + 1


def _attn_b_kernel(bias_ref, q_ref, k_ref, vt_ref, o_ref, m_ref, l_ref, acc_ref):
    tq = ATTN_TQ
    hd = B_HEAD_DIM
    qi = pl.program_id(2)
    q = q_ref[...]
    lane = lax.broadcasted_iota(jnp.int32, q.shape, 1)
    is_head0 = lane < hd
    zero = jnp.zeros_like(q)
    q_heads = (jnp.where(is_head0, q, zero), jnp.where(is_head0, zero, q))
    _init_softmax_state(m_ref, l_ref, acc_ref)
    for j in reversed(range(BAND_BLOCKS)):
        blk = qi - (BAND_BLOCKS - 1) + j
        start = pl.multiple_of(jnp.maximum(blk, 0) * tq, tq)
        k = k_ref[pl.ds(start, tq), :]
        vt = vt_ref[:, pl.ds(start, tq)]
        for h in range(2):
            s = lax.dot_general(k, q_heads[h], NT_DIMS, preferred_element_type=F32) + bias_ref[h, j]
            if j < BAND_BLOCKS - 1:
                s = jnp.where(blk >= 0, s, MASKED)
            _softmax_step(s, vt[h * hd:(h + 1) * hd, :], m_ref, l_ref, acc_ref, h)
    o = jnp.concatenate([acc_ref[h] * (1.0 / l_ref[h]) for h in range(2)], axis=0)
    o_ref[...] = o.T.astype(o_ref.dtype)


def _band_bias(rel_table):
    tq = ATTN_TQ
    kpos = (jnp.arange(BAND_BLOCKS * tq) - (BAND_BLOCKS - 1) * tq)[:, None]
    qpos = jnp.arange(tq)[None, :]
    idx = jnp.clip(kpos - qpos, -MAX_REL_DIST, MAX_REL_DIST) + MAX_REL_DIST
    kc = jnp.floor_divide(kpos, CHUNK)
    qc = qpos // CHUNK
    in_band = (kc <= qc) & (kc >= qc - LEFT_CHUNKS)
    bias = jnp.where(in_band[None], rel_table.astype(F32)[:, idx], MASKED)
    return bias.reshape(B_HEADS, BAND_BLOCKS, tq, tq)


def _band_attention(bias, q, k, vt):
    t = q.shape[0]
    b = t // SEQ
    tq = ATTN_TQ
    nq = SEQ // tq
    hp = B_HEADS // 2
    return pl.pallas_call(
        _attn_b_kernel,
        grid=(hp, b, nq),
        in_specs=[
            pl.BlockSpec((2, BAND_BLOCKS, tq, tq), lambda h, bi, qi: (h, 0, 0, 0)),
            pl.BlockSpec((tq, LANES), lambda h, bi, qi: (bi * nq + qi, h)),
            pl.BlockSpec((SEQ, LANES), lambda h, bi, qi: (bi, h)),
            pl.BlockSpec((None, LANES, SEQ), lambda h, bi, qi: (bi, h, 0)),
        ],
        out_specs=pl.BlockSpec((tq, LANES), lambda h, bi, qi: (bi * nq + qi, h)),
        out_shape=jax.ShapeDtypeStruct((t, D_MODEL), BF16),
        scratch_shapes=[
            pltpu.VMEM((2, 1, tq), F32),
            pltpu.VMEM((2, 1, tq), F32),
            pltpu.VMEM((2, B_HEAD_DIM, tq), F32),
        ],
        compiler_params=_params(("parallel", "parallel", "arbitrary"), 32),
        name="band_attention",
    )(bias, q, k, vt)


def _layer_norm_rows(y, g, b):
    mu = jnp.mean(y, axis=-1, keepdims=True)
    d = y - mu
    var = jnp.mean(d * d, axis=-1, keepdims=True)
    return d * lax.rsqrt(var + LN_EPS) * g + b


def _proj_ln_kernel(a_ref, w_ref, res_ref, g_ref, b_ref, o_ref):
    mix = jnp.dot(a_ref[...], w_ref[...], preferred_element_type=F32)
    y = DEEPNORM_ALPHA * res_ref[...] + mix
    o_ref[...] = _layer_norm_rows(y, g_ref[...], b_ref[...])


def _proj_ln(a, w, res, g, b, *, tm=512):
    t, d = res.shape
    row = pl.BlockSpec((1, d), lambda i: (0, 0))
    return pl.pallas_call(
        _proj_ln_kernel,
        grid=(t // tm,),
        in_specs=[
            pl.BlockSpec((tm, d), lambda i: (i, 0)),
            pl.BlockSpec((d, d), lambda i: (0, 0)),
            pl.BlockSpec((tm, d), lambda i: (i, 0)),
            row, row,
        ],
        out_specs=pl.BlockSpec((tm, d), lambda i: (i, 0)),
        out_shape=jax.ShapeDtypeStruct((t, d), F32),
        compiler_params=_params(("parallel",), 32),
        name="proj_ln",
    )(a, w, res, g.reshape(1, d), b.reshape(1, d))


FFN_CHUNKS = ((0, 1024), (1024, 2048), (2048, D_FF))


def _ffn_ln_kernel(x_ref, win_ref, wout_ref, g_ref, b_ref, o_ref, act_ref):
    x = x_ref[...]
    xb = x.astype(BF16)
    for c0, c1 in FFN_CHUNKS:
        gate = jnp.dot(xb, win_ref[:, c0:c1], preferred_element_type=F32)
        up = jnp.dot(xb, win_ref[:, D_FF + c0:D_FF + c1], preferred_element_type=F32)
        act_ref[:, c0:c1] = (gate * jax.nn.sigmoid(gate) * up).astype(BF16)
    y = DEEPNORM_ALPHA * x + jnp.dot(act_ref[...], wout_ref[...], preferred_element_type=F32)
    o_ref[...] = _layer_norm_rows(y, g_ref[...], b_ref[...])


def _ffn_ln(x2, w_in, w_out, g, b, *, tm=512):
    t, d = x2.shape
    row = pl.BlockSpec((1, d), lambda i: (0, 0))
    resident = functools.partial(pl.BlockSpec, pipeline_mode=pl.Buffered(1))
    return pl.pallas_call(
        _ffn_ln_kernel,
        grid=(t // tm,),
        in_specs=[
            pl.BlockSpec((tm, d), lambda i: (i, 0)),
            resident((d, 2 * D_FF), lambda i: (0, 0)),
            resident((D_FF, d), lambda i: (0, 0)),
            row, row,
        ],
        out_specs=pl.BlockSpec((tm, d), lambda i: (i, 0)),
        out_shape=jax.ShapeDtypeStruct((t, d), F32),
        scratch_shapes=[pltpu.VMEM((tm, D_FF), BF16)],
        compiler_params=_params(("parallel",), 56),
        name="ffn_ln",
    )(x2, w_in, w_out, g.reshape(1, d), b.reshape(1, d))


def _rope_tables():
    half = A_HEAD_DIM // 2
    inv = ROPE_THETA ** (-jnp.arange(0, A_HEAD_DIM, 2, dtype=F32) / A_HEAD_DIM)
    ang = jnp.arange(SEQ, dtype=F32)[:, None] * inv[None, :]
    cos = jnp.tile(jnp.cos(ang), (1, LANES // half))
    sin = jnp.sin(ang)
    sin = jnp.concatenate([-sin, -sin, sin, sin], axis=1)
    scale = A_HEAD_DIM ** -0.5
    return jnp.stack([cos * scale, cos]), jnp.stack([sin * scale, sin])


def _slab_permute(w):
    d = w.shape[0]
    half = A_HEAD_DIM // 2
    return w.reshape(d, A_HEADS, 2, 2, half).transpose(0, 1, 3, 2, 4).reshape(d, D_MODEL)


def kernel(x, a_w_qkv, a_lambda, a_subln_g, a_w_o, kv_w, b_w_q, b_rel_bias, b_w_o, ln_g, ln_b, ffn_w_in, ffn_w_out):
    bsz, seq, d = x.shape
    assert (seq, d) == (SEQ, D_MODEL)
    h = x.reshape(bsz * seq, d)
    rope = _rope_tables()
    for l in range(DEPTH):
        if l < N_A_LAYERS:
            wq, wk, wv = jnp.split(a_w_qkv[l], 3, axis=1)
            w_qk = jnp.concatenate([_slab_permute(wq), _slab_permute(wk)], axis=1).astype(BF16)
            qk = _project(h, w_qk, rope=rope)
            vt = _project_vt(h, wv.T.astype(BF16))
            lambda_init = 0.8 - 0.6 * math.exp(-0.3 * l)
            mix = _diff_attention(a_lambda[l], a_subln_g[l], qk, vt, lambda_init)
            w_o = a_w_o[l]
        else:
            j = l - N_A_LAYERS
            if l == N_A_LAYERS:
                wk, wv = jnp.split(kv_w, 2, axis=1)
                k_shared = _project(h, wk.astype(BF16))
                vt = _project_vt(h, wv.T.astype(BF16))
            q = _project(h, (b_w_q[j] * (B_HEAD_DIM ** -0.5)).astype(BF16))
            mix = _band_attention(_band_bias(b_rel_bias[j]), q, k_shared, vt)
            w_o = b_w_o[j]
        h = _proj_ln(mix, w_o.astype(BF16), h, ln_g[l, 0], ln_b[l, 0])
        h = _ffn_ln(h, ffn_w_in[l].astype(BF16), ffn_w_out[l].astype(BF16), ln_g[l, 1], ln_b[l, 1])
    return h.reshape(bsz, seq, d)
```
